```python
import jax, jax.numpy as jnp
from jax import lax
import numpy as np

D_MODEL = 2048
BATCH = 1
SEQ = 8192
DEPTH = 4

N_META = 16
N_HEADS = 16
HEAD_DIM = 128
D_ATTN = N_HEADS * HEAD_DIM
KV_RANK = 256
IDX_HEADS = 8
IDX_DIM = 64
TOPK_MAX = 256
POOL_WINDOWS = (2, 4, 8, 16)
N_POOL_GROUPS = 4
D_POOL = D_MODEL // 2
POOL_GROUP = D_POOL // N_POOL_GROUPS
D_FF = 4 * D_MODEL
Q_BLOCK = 128
EPS = 1e-6
SPLIT_SIZES = (D_ATTN, KV_RANK, IDX_HEADS * IDX_DIM, IDX_DIM, IDX_HEADS, D_POOL, D_MODEL, D_MODEL)
N_IN = sum(SPLIT_SIZES)

kernel_name = "hybrid_dsa_multiscale_pool_gated"


def rms_norm(x, g):
    xf = x.astype(jnp.float32)
    y = xf * lax.rsqrt(jnp.mean(xf * xf, axis=-1, keepdims=True) + EPS)
    return (y * g.astype(jnp.float32)).astype(x.dtype)


def dsa_attention(q, c_kv, q_idx, k_idx, w_idx, w_uk, w_uv, n_keys):
    B, TP = q.shape[0], q.shape[1]
    topk = min(TOPK_MAX, n_keys // 4)
    nb = TP // Q_BLOCK
    q_lat = jnp.einsum('bthe,hce->bthc', q, w_uk)
    key_pos = jnp.arange(TP)
    batch_ix = jnp.arange(B)[:, None, None]

    def to_blocks(a):
        return a.reshape((B, nb, Q_BLOCK) + a.shape[2:]).swapaxes(0, 1)

    def block(args):
        ql, qi, wi, start = args
        qpos = start + jnp.arange(Q_BLOCK)
        causal = key_pos[None, :] <= qpos[:, None]
        logits = jnp.einsum('bqhd,bsd->bqhs', qi, k_idx).astype(jnp.float32) * (IDX_DIM ** -0.5)
        score = jnp.einsum('bqh,bqhs->bqs', wi.astype(jnp.float32), jax.nn.relu(logits))
        score = jnp.where(causal[None], score, -jnp.inf)
        _, idx = lax.top_k(score, topk)
        sel = c_kv[batch_ix, idx]
        s = jnp.einsum('bqhr,bqkr->bqhk', ql, sel).astype(jnp.float32) * (HEAD_DIM ** -0.5)
        valid = idx <= qpos[None, :, None]
        s = jnp.where(valid[:, :, None, :], s, -jnp.inf)
        p = jax.nn.softmax(s, axis=-1).astype(sel.dtype)
        return jnp.einsum('bqhk,bqkr->bqhr', p, sel)

    starts = jnp.arange(nb) * Q_BLOCK
    o_lat = lax.map(block, (to_blocks(q_lat), to_blocks(q_idx), to_blocks(w_idx), starts))
    o_lat = o_lat.swapaxes(0, 1).reshape(B, TP, N_HEADS, KV_RANK)
    o = jnp.einsum('bthc,hce->bthe', o_lat, w_uv)
    return o.reshape(B, TP, D_ATTN)


def multiscale_pool(p, w_pool, scale):
    B, TP = p.shape[0], p.shape[1]
    pg = p.astype(jnp.float32).reshape(B, TP, N_POOL_GROUPS, POOL_GROUP)
    csum = jnp.concatenate([jnp.zeros_like(pg[:, :1]), jnp.cumsum(pg, axis=1)], axis=1)
    t1 = jnp.arange(1, TP + 1)[:, None]
    win = jnp.array(POOL_WINDOWS, dtype=jnp.int32)[None, :]
    lo = jnp.maximum(t1 - win, 0)
    gix = jnp.arange(N_POOL_GROUPS)[None, :]
    window_sum = csum[:, 1:] - csum[:, lo, gix]
    count = (t1 - lo).astype(jnp.float32)[None, :, :, None]
    y = (window_sum / count - pg).astype(p.dtype)
    y = jnp.einsum('btgc,gcd->btgd', y, w_pool).reshape(B, TP, D_POOL)
    return y * scale


def setup_inputs(seed: int = 0) -> dict:
    key = jax.random.key(seed)
    ks = jax.random.split(key, 17)
    f32 = jnp.float32

    def nrm(k, shape, fan_in):
        return jax.random.normal(k, shape, f32) * (fan_in ** -0.5)

    def gain(k, shape):
        return 1.0 + 0.02 * jax.random.normal(k, shape, f32)

    return {
        "x": jax.random.normal(ks[0], (BATCH, SEQ, D_MODEL), f32),
        "meta_tokens": jax.random.normal(ks[1], (N_META, D_MODEL), f32),
        "norm_mix_g": gain(ks[2], (DEPTH, D_MODEL)),
        "w_in": nrm(ks[3], (DEPTH, D_MODEL, N_IN), D_MODEL),
        "kv_norm_g": gain(ks[4], (DEPTH, KV_RANK)),
        "idx_k_norm_g": gain(ks[5], (DEPTH, IDX_DIM)),
        "w_uk": nrm(ks[6], (DEPTH, N_HEADS, KV_RANK, HEAD_DIM), KV_RANK),
        "w_uv": nrm(ks[7], (DEPTH, N_HEADS, KV_RANK, HEAD_DIM), KV_RANK),
        "w_attn_o": nrm(ks[8], (DEPTH, D_ATTN, D_MODEL), D_ATTN),
        "w_pool": nrm(ks[9], (DEPTH, N_POOL_GROUPS, POOL_GROUP, POOL_GROUP), POOL_GROUP),
        "pool_scale": gain(ks[10], (DEPTH, D_POOL)),
        "w_pool_o": nrm(ks[11], (DEPTH, D_POOL, D_MODEL), D_POOL),
        "w_out": nrm(ks[12], (DEPTH, D_MODEL, D_MODEL), D_MODEL),
        "norm_mlp_g": gain(ks[13], (DEPTH, D_MODEL)),
        "w_mlp_in": nrm(ks[14], (DEPTH, D_MODEL, D_FF), D_MODEL),
        "w_mlp_out": nrm(ks[15], (DEPTH, D_FF, D_MODEL), D_FF),
        "final_norm_g": gain(ks[16], (D_MODEL,)),
    }


def reference(x, meta_tokens, norm_mix_g, w_in, kv_norm_g, idx_k_norm_g, w_uk, w_uv, w_attn_o,
              w_pool, pool_scale, w_pool_o, w_out, norm_mlp_g, w_mlp_in, w_mlp_out, final_norm_g):
    B, S, D = x.shape
    n_keys = S + N_META
    tp = -(-n_keys // Q_BLOCK) * Q_BLOCK
    meta = jnp.broadcast_to(meta_tokens.astype(x.dtype)[None], (B, N_META, D))
    pad = jnp.zeros((B, tp - n_keys, D), x.dtype)
    h_res = jnp.concatenate([meta, x, pad], axis=1)
    split_points = [int(v) for v in np.cumsum(SPLIT_SIZES)[:-1]]

    for l in range(DEPTH):
        h = rms_norm(h_res, norm_mix_g[l])
        proj = h @ w_in[l]
        q, c_kv, q_idx, k_idx, w_idx, p_in, g_a, g_b = jnp.split(proj, split_points, axis=-1)
        q = q.reshape(B, tp, N_HEADS, HEAD_DIM)
        c_kv = rms_norm(c_kv, kv_norm_g[l])
        q_idx = q_idx.reshape(B, tp, IDX_HEADS, IDX_DIM)
        k_idx = rms_norm(k_idx, idx_k_norm_g[l])
        w_idx = w_idx * (IDX_HEADS ** -0.5)
        a = dsa_attention(q, c_kv, q_idx, k_idx, w_idx, w_uk[l], w_uv[l], n_keys) @ w_attn_o[l]
        b = multiscale_pool(p_in, w_pool[l], pool_scale[l]) @ w_pool_o[l]
        merged = jax.nn.sigmoid(g_a) * a + jax.nn.sigmoid(g_b) * b
        h_res = h_res + merged @ w_out[l]
        h2 = rms_norm(h_res, norm_mlp_g[l])
        h_res = h_res + jnp.square(jax.nn.relu(h2 @ w_mlp_in[l])) @ w_mlp_out[l]

    y = rms_norm(h_res, final_norm_g)
    return y[:, N_META:N_META + S]
```

```python
import functools
import math

import jax
import jax.numpy as jnp
from jax import lax
from jax.experimental import pallas as pl
from jax.experimental.pallas import tpu as pltpu

EPS = 1e-6
TOPK_MAX = 256
Q_BLOCK = 128
POOL_WINDOWS = (2, 4, 8, 16)
LANES = 128
INT_MIN = -(2 ** 31)
MASKED_SCORE = -1e30
VMEM_LIMIT_BYTES = 56 * 1024 * 1024

BF16 = jnp.bfloat16
F32 = jnp.float32


def _cparams(*semantics):
    return pltpu.CompilerParams(dimension_semantics=semantics, vmem_limit_bytes=VMEM_LIMIT_BYTES)


def _rms(x, g):
    return x * lax.rsqrt(jnp.mean(x * x, axis=-1, keepdims=True) + EPS) * g


def _dot(a, b):
    return jnp.dot(a, b, preferred_element_type=F32)


def _dot_nt(a, b):
    return lax.dot_general(a, b, (((1,), (1,)), ((), ())), preferred_element_type=F32)


def _pick_block(total, target, quantum):
    best = None
    for cand in range(quantum, total + 1, quantum):
        if total % cand == 0 and cand <= target:
            best = cand
    assert best is not None, (total, target, quantum)
    return best


def _proj_small_body(x_ref, g_ref, w_ref, kvg_ref, kig_ref, h_ref, ckv_ref, qidx_ref, kidx_ref, widx_ref,
                     *, kv_rank, idx_heads, idx_dim):
    h = _rms(x_ref[...], g_ref[...]).astype(BF16)
    h_ref[...] = h
    pr = _dot(h, w_ref[...])
    ckv_ref[...] = _rms(pr[:, :kv_rank], kvg_ref[...]).astype(BF16)
    q_end = kv_rank + idx_heads * LANES
    qidx_ref[...] = pr[:, kv_rank:q_end].astype(BF16)
    kraw = pr[:, q_end:q_end + LANES]
    ms = jnp.sum(kraw * kraw, axis=-1, keepdims=True) * (1.0 / idx_dim)
    kidx_ref[...] = (kraw * lax.rsqrt(ms + EPS) * kig_ref[...]).astype(BF16)
    widx_ref[...] = pr[:, q_end + LANES:q_end + 2 * LANES] * (idx_heads ** -0.5)


def _proj_small(x, g, w_small, kvg, kig_pad, *, tm, kv_rank, idx_heads, idx_dim):
    tp, d = x.shape
    ns = w_small.shape[1]
    body = functools.partial(_proj_small_body, kv_rank=kv_rank, idx_heads=idx_heads, idx_dim=idx_dim)
    row = lambda i: (i, 0)
    const = lambda i: (0, 0)
    return pl.pallas_call(
        body,
        grid=(tp // tm,),
        in_specs=[pl.BlockSpec((tm, d), row), pl.BlockSpec((1, d), const), pl.BlockSpec((d, ns), const),
                  pl.BlockSpec((1, kv_rank), const), pl.BlockSpec((1, LANES), const)],
        out_specs=[pl.BlockSpec((tm, d), row), pl.BlockSpec((tm, kv_rank), row),
                   pl.BlockSpec((tm, idx_heads * LANES), row), pl.BlockSpec((tm, LANES), row),
                   pl.BlockSpec((tm, LANES), row)],
        out_shape=[jax.ShapeDtypeStruct((tp, d), BF16), jax.ShapeDtypeStruct((tp, kv_rank), BF16),
                   jax.ShapeDtypeStruct((tp, idx_heads * LANES), BF16), jax.ShapeDtypeStruct((tp, LANES), BF16),
                   jax.ShapeDtypeStruct((tp, LANES), F32)],
        compiler_params=_cparams("arbitrary"),
        name="proj_small",
    )(x, g, w_small, kvg, kig_pad)


def _mm_body(x_ref, w_ref, o_ref):
    o_ref[...] = _dot(x_ref[...], w_ref[...]).astype(o_ref.dtype)


def _mm(x, w, *, tm, out_dtype, name):
    m, k = x.shape
    n = w.shape[1]
    tn = _pick_block(n, 512, LANES)
    return pl.pallas_call(
        _mm_body,
        grid=(m // tm, n // tn),
        in_specs=[pl.BlockSpec((tm, k), lambda i, j: (i, 0)), pl.BlockSpec((k, tn), lambda i, j: (0, j))],
        out_specs=pl.BlockSpec((tm, tn), lambda i, j: (i, j)),
        out_shape=jax.ShapeDtypeStruct((m, n), out_dtype),
        compiler_params=_cparams("arbitrary", "arbitrary"),
        name=name,
    )(x, w)


def _qlat_body(h_ref, wq_ref, wuk_ref, o_ref, *, head_dim, heads_per_step):
    q = _dot(h_ref[...], wq_ref[...]).astype(BF16)
    nqb, _, qb, rank = o_ref.shape
    for t in range(heads_per_step):
        ql = _dot_nt(q[:, t * head_dim:(t + 1) * head_dim], wuk_ref[t])
        o_ref[:, t, :, :] = ql.astype(BF16).reshape(nqb, qb, rank)


def _qlat(h, wq, w_uk, *, tm, qb):
    tp, d = h.shape
    n_heads, rank, head_dim = w_uk.shape
    hps = 2 if n_heads % 2 == 0 else 1
    body = functools.partial(_qlat_body, head_dim=head_dim, heads_per_step=hps)
    return pl.pallas_call(
        body,
        grid=(tp // tm, n_heads // hps),
        in_specs=[pl.BlockSpec((tm, d), lambda i, j: (i, 0)),
                  pl.BlockSpec((d, hps * head_dim), lambda i, j: (0, j)),
                  pl.BlockSpec((hps, rank, head_dim), lambda i, j: (j, 0, 0))],
        out_specs=pl.BlockSpec((tm // qb, hps, qb, rank), lambda i, j: (i, j, 0, 0)),
        out_shape=jax.ShapeDtypeStruct((tp // qb, n_heads, qb, rank), BF16),
        compiler_params=_cparams("arbitrary", "arbitrary"),
        name="q_latent",
    )(h, wq, w_uk)


def _order_key(score):
    bits = pltpu.bitcast(score, jnp.int32)
    return bits ^ ((bits >> 31) & jnp.int32(0x7FFFFFFF))


def _attn_body(qidx_ref, widx_ref, qlat_ref, kidx_ref, ckv_ref, o_ref,
               keys_ref, wb_ref, m_ref, l_ref, acc_ref,
               *, idx_heads, idx_dim, topk, kc, sm_scale):
    i = pl.program_id(0)
    n_heads, qb, rank = qlat_ref.shape[1:]
    rows = n_heads * qb
    n_chunks = ((i + 1) * qb + (kc - 1)) // kc
    q_pos = i * qb + lax.broadcasted_iota(jnp.int32, (qb, kc), 0)
    k_off = lax.broadcasted_iota(jnp.int32, (qb, kc), 1)

    w_all = widx_ref[...] * (idx_dim ** -0.5)
    for hh in range(idx_heads):
        wb_ref[hh] = jnp.broadcast_to(w_all[:, hh:hh + 1], (qb, LANES))

    def score_chunk(c, carry):
        k_chunk = kidx_ref[pl.ds(pl.multiple_of(c * kc, kc), kc), :]
        sc = jnp.zeros((qb, kc), F32)
        for hh in range(idx_heads):
            lg = _dot_nt(qidx_ref[:, hh * LANES:(hh + 1) * LANES], k_chunk)
            sc = sc + jnp.maximum(lg, 0.0) * pltpu.repeat(wb_ref[hh], kc // LANES, axis=1)
        causal = (c * kc + k_off) <= q_pos
        keys_ref[c] = jnp.where(causal, _order_key(sc), jnp.int32(INT_MIN))
        return carry

    lax.fori_loop(0, n_chunks, score_chunk, 0)

    def bit_step(b, t_cur):
        cand = t_cur + lax.shift_left(jnp.int32(1), jnp.int32(31) - b)
        cand_w = pltpu.repeat(cand, kc // LANES, axis=1)

        def count_chunk(c, acc):
            ge = jnp.where(keys_ref[c] >= cand_w, 1.0, 0.0)
            for t in range(kc // LANES):
                acc = acc + ge[:, t * LANES:(t + 1) * LANES]
            return acc

        acc = lax.fori_loop(0, n_chunks, count_chunk, jnp.zeros((qb, LANES), F32))
        cnt = jnp.sum(acc, axis=1, keepdims=True)
        return jnp.where(cnt >= float(topk), cand, t_cur)

    thr = lax.fori_loop(0, 32, bit_step, jnp.full((qb, LANES), INT_MIN, jnp.int32))
    thr_w = pltpu.repeat(jnp.maximum(thr, jnp.int32(INT_MIN + 1)), kc // LANES, axis=1)

    m_ref[...] = jnp.full(m_ref.shape, MASKED_SCORE, F32)
    l_ref[...] = jnp.zeros(l_ref.shape, F32)
    acc_ref[...] = jnp.zeros(acc_ref.shape, F32)
    q_all = qlat_ref[0].reshape(rows, rank)
    c_exp = sm_scale * math.log2(math.e)

    def attn_chunk(c, carry):
        kv = ckv_ref[pl.ds(pl.multiple_of(c * kc, kc), kc), :]
        sel = keys_ref[c] >= thr_w
        s = _dot_nt(q_all, kv).reshape(n_heads, qb, kc)
        s = jnp.where(sel[None], s, MASKED_SCORE).reshape(rows, kc)
        m_prev = m_ref[...]
        m_next = jnp.maximum(m_prev, jnp.max(s, axis=1, keepdims=True))
        p = jnp.exp2((s - pltpu.repeat(m_next, kc // LANES, axis=1)) * c_exp)
        alpha = jnp.exp2((m_prev - m_next) * c_exp)
        l_ref[...] = alpha * l_ref[...] + jnp.sum(p, axis=1, keepdims=True)
        m_ref[...] = m_next
        acc_ref[...] = acc_ref[...] * pltpu.repeat(alpha, rank // LANES, axis=1) + _dot(p.astype(BF16), kv)
        return carry

    lax.fori_loop(0, n_chunks, attn_chunk, 0)
    out = acc_ref[...] / pltpu.repeat(l_ref[...], rank // LANES, axis=1)
    o_ref[0] = out.astype(BF16).reshape(n_heads, qb, rank)


def _attention(qidx, widx, qlat4, kidx, ckv, *, idx_heads, idx_dim, topk, kc, sm_scale):
    nqb, n_heads, qb, rank = qlat4.shape
    tp = ckv.shape[0]
    body = functools.partial(_attn_body, idx_heads=idx_heads, idx_dim=idx_dim, topk=topk, kc=kc, sm_scale=sm_scale)
    return pl.pallas_call(
        body,
        grid=(nqb,),
        in_specs=[pl.BlockSpec((qb, idx_heads * LANES), lambda i: (i, 0)),
                  pl.BlockSpec((qb, LANES), lambda i: (i, 0)),
                  pl.BlockSpec((1, n_heads, qb, rank), lambda i: (i, 0, 0, 0)),
                  pl.BlockSpec((tp, LANES), lambda i: (0, 0)),
                  pl.BlockSpec((tp, rank), lambda i: (0, 0))],
        out_specs=pl.BlockSpec((1, n_heads, qb, rank), lambda i: (i, 0, 0, 0)),
        out_shape=jax.ShapeDtypeStruct((nqb, n_heads, qb, rank), BF16),
        scratch_shapes=[pltpu.VMEM((tp // kc, qb, kc), jnp.int32),
                        pltpu.VMEM((idx_heads, qb, LANES), F32),
                        pltpu.VMEM((n_heads * qb, LANES), F32),
                        pltpu.VMEM((n_heads * qb, LANES), F32),
                        pltpu.VMEM((n_heads * qb, rank), F32)],
        compiler_params=_cparams("arbitrary"),
        name="dsa_attention",
    )(qidx, widx, qlat4, kidx, ckv)


def _ov_body(olat_ref, wuv_ref, o_ref, *, head_dim, heads_per_step):
    nqb, _, qb, rank = olat_ref.shape
    for t in range(heads_per_step):
        x = olat_ref[:, t, :, :].reshape(nqb * qb, rank)
        o_ref[:, t * head_dim:(t + 1) * head_dim] = _dot(x, wuv_ref[t]).astype(BF16)


def _value_up(olat4, w_uv, *, tm):
    nqb, n_heads, qb, rank = olat4.shape
    head_dim = w_uv.shape[2]
    tp = nqb * qb
    hps = 2 if n_heads % 2 == 0 else 1
    body = functools.partial(_ov_body, head_dim=head_dim, heads_per_step=hps)
    return pl.pallas_call(
        body,
        grid=(tp // tm, n_heads // hps),
        in_specs=[pl.BlockSpec((tm // qb, hps, qb, rank), lambda i, j: (i, j, 0, 0)),
                  pl.BlockSpec((hps, rank, head_dim), lambda i, j: (j, 0, 0))],
        out_specs=pl.BlockSpec((tm, hps * head_dim), lambda i, j: (i, j)),
        out_shape=jax.ShapeDtypeStruct((tp, n_heads * head_dim), BF16),
        compiler_params=_cparams("arbitrary", "arbitrary"),
        name="value_up",
    )(olat4, w_uv)


def _pool_body(p_ref, w_ref, s_ref, o_ref, buf_ref, *, halo, group):
    i = pl.program_id(0)
    tm = p_ref.shape[0]

    @pl.when(i == 0)
    def _():
        buf_ref[0:halo, :] = jnp.zeros((halo, buf_ref.shape[1]), F32)

    @pl.when(i > 0)
    def _():
        buf_ref[0:halo, :] = buf_ref[tm:tm + halo, :]

    buf_ref[halo:halo + tm, :] = p_ref[...]
    t1 = i * tm + lax.broadcasted_iota(jnp.int32, (tm, 1), 0) + 1
    for g, win in enumerate(POOL_WINDOWS):
        cols = slice(g * group, (g + 1) * group)
        cur = buf_ref[halo:halo + tm, cols]
        wsum = cur
        for j in range(1, win):
            wsum = wsum + buf_ref[halo - j:halo - j + tm, cols]
        count = jnp.minimum(t1, win).astype(F32)
        y = (wsum / count - cur).astype(BF16)
        o_ref[:, cols] = (_dot(y, w_ref[g]) * s_ref[:, cols]).astype(BF16)


def _pool(pg, w_pool, scale, *, tm, col_block):
    tp = pg.shape[0]
    n_groups, group, _ = w_pool.shape
    d_pool = n_groups * group
    halo = max(POOL_WINDOWS)
    body = functools.partial(_pool_body, halo=halo, group=group)
    return pl.pallas_call(
        body,
        grid=(tp // tm,),
        in_specs=[pl.BlockSpec((tm, d_pool), lambda i: (i, col_block)),
                  pl.BlockSpec((n_groups, group, group), lambda i: (0, 0, 0)),
                  pl.BlockSpec((1, d_pool), lambda i: (0, 0))],
        out_specs=pl.BlockSpec((tm, d_pool), lambda i: (i, 0)),
        out_shape=jax.ShapeDtypeStruct((tp, d_pool), BF16),
        scratch_shapes=[pltpu.VMEM((tm + halo, d_pool), F32)],
        compiler_params=_cparams("arbitrary"),
        name="multiscale_pool",
    )(pg, w_pool, scale)


def _merge_body(a_ref, b_ref, ga_ref, gb_ref, x_ref, w_ref, o_ref, merged_ref):
    @pl.when(pl.program_id(1) == 0)
    def _():
        merged = jax.nn.sigmoid(ga_ref[...]) * a_ref[...] + jax.nn.sigmoid(gb_ref[...]) * b_ref[...]
        merged_ref[...] = merged.astype(BF16)

    o_ref[...] = x_ref[...] + _dot(merged_ref[...], w_ref[...])


def _merge_out(a, b, pg, x, w_out, *, tm, tn):
    tp, d = x.shape
    return pl.pallas_call(
        _merge_body,
        grid=(tp // tm, d // tn),
        in_specs=[pl.BlockSpec((tm, d), lambda i, j: (i, 0)), pl.BlockSpec((tm, d), lambda i, j: (i, 0)),
                  pl.BlockSpec((tm, d), lambda i, j: (i, 0)), pl.BlockSpec((tm, d), lambda i, j: (i, 1)),
                  pl.BlockSpec((tm, tn), lambda i, j: (i, j)), pl.BlockSpec((d, tn), lambda i, j: (0, j))],
        out_specs=pl.BlockSpec((tm, tn), lambda i, j: (i, j)),
        out_shape=jax.ShapeDtypeStruct((tp, d), F32),
        scratch_shapes=[pltpu.VMEM((tm, d), BF16)],
        compiler_params=_cparams("arbitrary", "arbitrary"),
        name="merge_out",
    )(a, b, pg, pg, x, w_out)


def _mlp_body(x_ref, g_ref, w1_ref, w2_ref, o_ref, h_ref):
    @pl.when(pl.program_id(1) == 0)
    def _():
        x = x_ref[...]
        h_ref[...] = _rms(x, g_ref[...]).astype(BF16)
        o_ref[...] = x

    u = jnp.maximum(_dot(h_ref[...], w1_ref[...]), 0.0)
    o_ref[...] += _dot((u * u).astype(BF16), w2_ref[...])


def _mlp(x, g, w1, w2, *, tm, tf):
    tp, d = x.shape
    d_ff = w1.shape[1]
    return pl.pallas_call(
        _mlp_body,
        grid=(tp // tm, d_ff // tf),
        in_specs=[pl.BlockSpec((tm, d), lambda i, f: (i, 0)), pl.BlockSpec((1, d), lambda i, f: (0, 0)),
                  pl.BlockSpec((d, tf), lambda i, f: (0, f)), pl.BlockSpec((tf, d), lambda i, f: (f, 0))],
        out_specs=pl.BlockSpec((tm, d), lambda i, f: (i, 0)),
        out_shape=jax.ShapeDtypeStruct((tp, d), F32),
        scratch_shapes=[pltpu.VMEM((tm, d), BF16)],
        compiler_params=_cparams("arbitrary", "arbitrary"),
        name="mlp_relu2",
    )(x, g, w1, w2)


def _final_norm_body(x_ref, g_ref, o_ref):
    o_ref[...] = _rms(x_ref[...], g_ref[...])


def _final_norm(x, g, *, tm):
    tp, d = x.shape
    return pl.pallas_call(
        _final_norm_body,
        grid=(tp // tm,),
        in_specs=[pl.BlockSpec((tm, d), lambda i: (i, 0)), pl.BlockSpec((1, d), lambda i: (0, 0))],
        out_specs=pl.BlockSpec((tm, d), lambda i: (i, 0)),
        out_shape=jax.ShapeDtypeStruct((tp, d), F32),
        compiler_params=_cparams("arbitrary"),
        name="final_norm",
    )(x, g)


def kernel(x, meta_tokens, norm_mix_g, w_in, kv_norm_g, idx_k_norm_g, w_uk, w_uv, w_attn_o, w_pool, pool_scale,
           w_pool_o, w_out, norm_mlp_g, w_mlp_in, w_mlp_out, final_norm_g):
    batch, seq, d = x.shape
    assert batch == 1
    depth = w_in.shape[0]
    n_meta = meta_tokens.shape[0]
    n_heads, kv_rank, head_dim = w_uk.shape[1:]
    d_attn = n_heads * head_dim
    idx_dim = idx_k_norm_g.shape[1]
    d_pool = pool_scale.shape[1]
    n_in = w_in.shape[2]
    idx_heads = (n_in - d_attn - kv_rank - idx_dim - d_pool - 2 * d) // (idx_dim + 1)
    assert d_attn + kv_rank + idx_heads * idx_dim + idx_dim + idx_heads + d_pool + 2 * d == n_in
    assert idx_dim <= LANES and idx_heads <= LANES and kv_rank % LANES == 0 and d % LANES == 0
    assert d_pool % d == 0 or d % d_pool == 0

    n_keys = seq + n_meta
    tp = -(-n_keys // Q_BLOCK) * Q_BLOCK
    topk = min(TOPK_MAX, n_keys // 4)
    qb = Q_BLOCK
    tm = _pick_block(tp, 640, qb)
    kc = _pick_block(tp, 640, LANES)
    tn = _pick_block(d, 512, LANES)
    tf = _pick_block(w_mlp_in.shape[2], 512, LANES)

    h_res = jnp.concatenate([meta_tokens.astype(x.dtype), x[0], jnp.zeros((tp - n_keys, d), x.dtype)], axis=0)

    o_q, o_kv = 0, d_attn
    o_qi = o_kv + kv_rank
    o_ki = o_qi + idx_heads * idx_dim
    o_wi = o_ki + idx_dim
    o_p = o_wi + idx_heads
    o_ga = o_p + d_pool
    o_gb = o_ga + d

    for l in range(depth):
        wl = w_in[l]
        qi = wl[:, o_qi:o_ki].reshape(d, idx_heads, idx_dim)
        qi = jnp.pad(qi, ((0, 0), (0, 0), (0, LANES - idx_dim))).reshape(d, idx_heads * LANES)
        ki = jnp.pad(wl[:, o_ki:o_wi], ((0, 0), (0, LANES - idx_dim)))
        wi = jnp.pad(wl[:, o_wi:o_p], ((0, 0), (0, LANES - idx_heads)))
        w_small = jnp.concatenate([wl[:, o_kv:o_qi], qi, ki, wi], axis=1).astype(BF16)
        w_q = wl[:, o_q:o_kv].astype(BF16)
        w_pg = jnp.concatenate([wl[:, o_ga:o_gb], wl[:, o_gb:], wl[:, o_p:o_ga]], axis=1).astype(BF16)
        kig_pad = jnp.pad(idx_k_norm_g[l], (0, LANES - idx_dim))[None]

        h, ckv, qidx, kidx, widx = _proj_small(
            h_res, norm_mix_g[l][None], w_small, kv_norm_g[l][None], kig_pad,
            tm=tm, kv_rank=kv_rank, idx_heads=idx_heads, idx_dim=idx_dim)
        qlat4 = _qlat(h, w_q, w_uk[l].astype(BF16), tm=tm, qb=qb)
        pg = _mm(h, w_pg, tm=tm, out_dtype=F32, name="proj_gates_pool")

        olat4 = _attention(qidx, widx, qlat4, kidx, ckv, idx_heads=idx_heads, idx_dim=idx_dim, topk=topk,
                           kc=kc, sm_scale=head_dim ** -0.5)
        o = _value_up(olat4, w_uv[l].astype(BF16), tm=tm)
        a = _mm(o, w_attn_o[l].astype(BF16), tm=tm, out_dtype=F32, name="attn_out")

        yps = _pool(pg, w_pool[l].astype(BF16), pool_scale[l][None], tm=tm, col_block=(2 * d) // d_pool)
        b = _mm(yps, w_pool_o[l].astype(BF16), tm=tm, out_dtype=F32, name="pool_out")

        h_res = _merge_out(a, b, pg, h_res, w_out[l].astype(BF16), tm=tm, tn=tn)
        h_res = _mlp(h_res, norm_mlp_g[l][None], w_mlp_in[l].astype(BF16), w_mlp_out[l].astype(BF16), tm=tm, tf=tf)

    y = _final_norm(h_res, final_norm_g[None], tm=tm)
    return y[n_meta:n_meta + seq][None]
```

```python
import functools
import math

import jax
import jax.numpy as jnp
from jax import lax
from jax.experimental import pallas as pl
from jax.experimental.pallas import tpu as pltpu

EPS = 1e-6
TOPK_MAX = 256
Q_BLOCK = 128
POOL_WINDOWS = (2, 4, 8, 16)
LANES = 128
ROW_TILE = 32
HEAD_GROUP = 4
INT_MIN = -(2 ** 31)
MASKED_SCORE = -1e30
VMEM_LIMIT_BYTES = 56 * 1024 * 1024

BF16 = jnp.bfloat16
F32 = jnp.float32


def _cparams(*semantics):
    return pltpu.CompilerParams(dimension_semantics=semantics, vmem_limit_bytes=VMEM_LIMIT_BYTES)


def _rms(x, g):
    return x * lax.rsqrt(jnp.mean(x * x, axis=-1, keepdims=True) + EPS) * g


def _dot(a, b):
    return jnp.dot(a, b, preferred_element_type=F32)


def _dot_nt(a, b):
    return lax.dot_general(a, b, (((1,), (1,)), ((), ())), preferred_element_type=F32)


def _pick_block(total, target, quantum):
    best = None
    for cand in range(quantum, total + 1, quantum):
        if total % cand == 0 and cand <= target:
            best = cand
    assert best is not None, (total, target, quantum)
    return best


def _proj_small_body(x_ref, g_ref, w_ref, kvg_ref, kig_ref, h_ref, ckv_ref, qidx_ref, kidx_ref, widx_ref,
                     *, kv_rank, idx_heads, idx_dim):
    h = _rms(x_ref[...], g_ref[...]).astype(BF16)
    h_ref[...] = h
    pr = _dot(h, w_ref[...])
    ckv_ref[...] = _rms(pr[:, :kv_rank], kvg_ref[...]).astype(BF16)
    q_end = kv_rank + idx_heads * LANES
    qidx_ref[...] = pr[:, kv_rank:q_end].astype(BF16)
    kraw = pr[:, q_end:q_end + LANES]
    ms = jnp.sum(kraw * kraw, axis=-1, keepdims=True) * (1.0 / idx_dim)
    kidx_ref[...] = (kraw * lax.rsqrt(ms + EPS) * kig_ref[...]).astype(BF16)
    widx_ref[...] = pr[:, q_end + LANES:q_end + 2 * LANES] * (idx_heads ** -0.5)


def _proj_small(x, g, w_small, kvg, kig_pad, *, tm, kv_rank, idx_heads, idx_dim):
    tp, d = x.shape
    ns = w_small.shape[1]
    body = functools.partial(_proj_small_body, kv_rank=kv_rank, idx_heads=idx_heads, idx_dim=idx_dim)
    row = lambda i: (i, 0)
    const = lambda i: (0, 0)
    return pl.pallas_call(
        body,
        grid=(tp // tm,),
        in_specs=[pl.BlockSpec((tm, d), row), pl.BlockSpec((1, d), const), pl.BlockSpec((d, ns), const),
                  pl.BlockSpec((1, kv_rank), const), pl.BlockSpec((1, LANES), const)],
        out_specs=[pl.BlockSpec((tm, d), row), pl.BlockSpec((tm, kv_rank), row),
                   pl.BlockSpec((tm, idx_heads * LANES), row), pl.BlockSpec((tm, LANES), row),
                   pl.BlockSpec((tm, LANES), row)],
        out_shape=[jax.ShapeDtypeStruct((tp, d), BF16), jax.ShapeDtypeStruct((tp, kv_rank), BF16),
                   jax.ShapeDtypeStruct((tp, idx_heads * LANES), BF16), jax.ShapeDtypeStruct((tp, LANES), BF16),
                   jax.ShapeDtypeStruct((tp, LANES), F32)],
        compiler_params=_cparams("arbitrary"),
        name="proj_small",
    )(x, g, w_small, kvg, kig_pad)


def _mm_body(x_ref, w_ref, o_ref):
    o_ref[...] = _dot(x_ref[...], w_ref[...]).astype(o_ref.dtype)


def _mm(x, w, *, tm, out_dtype, name):
    m, k = x.shape
    n = w.shape[1]
    tn = _pick_block(n, 512, LANES)
    return pl.pallas_call(
        _mm_body,
        grid=(m // tm, n // tn),
        in_specs=[pl.BlockSpec((tm, k), lambda i, j: (i, 0)), pl.BlockSpec((k, tn), lambda i, j: (0, j))],
        out_specs=pl.BlockSpec((tm, tn), lambda i, j: (i, j)),
        out_shape=jax.ShapeDtypeStruct((m, n), out_dtype),
        compiler_params=_cparams("arbitrary", "arbitrary"),
        name=name,
    )(x, w)


def _qlat_body(h_ref, wq_ref, wuk_ref, o_ref, *, head_dim, heads_per_step):
    q = _dot(h_ref[...], wq_ref[...]).astype(BF16)
    nqb, _, qb, rank = o_ref.shape
    for t in range(heads_per_step):
        ql = _dot_nt(q[:, t * head_dim:(t + 1) * head_dim], wuk_ref[t])
        o_ref[:, t, :, :] = ql.astype(BF16).reshape(nqb, qb, rank)


def _qlat(h, wq, w_uk, *, tm, qb):
    tp, d = h.shape
    n_heads, rank, head_dim = w_uk.shape
    hps = 2 if n_heads % 2 == 0 else 1
    body = functools.partial(_qlat_body, head_dim=head_dim, heads_per_step=hps)
    return pl.pallas_call(
        body,
        grid=(tp // tm, n_heads // hps),
        in_specs=[pl.BlockSpec((tm, d), lambda i, j: (i, 0)),
                  pl.BlockSpec((d, hps * head_dim), lambda i, j: (0, j)),
                  pl.BlockSpec((hps, rank, head_dim), lambda i, j: (j, 0, 0))],
        out_specs=pl.BlockSpec((tm // qb, hps, qb, rank), lambda i, j: (i, j, 0, 0)),
        out_shape=jax.ShapeDtypeStruct((tp // qb, n_heads, qb, rank), BF16),
        compiler_params=_cparams("arbitrary", "arbitrary"),
        name="q_latent",
    )(h, wq, w_uk)


def _order_key(score):
    bits = pltpu.bitcast(score, jnp.int32)
    return bits ^ ((bits >> 31) & jnp.int32(0x7FFFFFFF))


def _attn_body(qidx_ref, widx_ref, qlat_ref, kidx_ref, ckv_ref, o_ref,
               keys_ref, wb_ref, qs_ref, lg_ref, s0_ref, s1_ref, p0_ref, p1_ref, a0_ref, a1_ref,
               m_ref, l_ref, acc_ref, *, idx_heads, idx_dim, topk, kc, sm_scale):
    i = pl.program_id(0)
    n_heads, qb, rank = qlat_ref.shape[1:]
    lane_tiles = kc // LANES
    rt = ROW_TILE
    grp = HEAD_GROUP
    n_groups = n_heads // grp
    rows_g = grp * qb
    n_chunks = ((i + 1) * qb + (kc - 1)) // kc

    w_all = widx_ref[...] * (idx_dim ** -0.5)
    for hh in range(idx_heads):
        wb_ref[hh] = jnp.broadcast_to(w_all[:, hh:hh + 1], (qb, LANES))
        qs_ref[hh * qb:(hh + 1) * qb, :] = qidx_ref[:, hh * LANES:(hh + 1) * LANES]

    def score_chunk(c, carry):
        k_chunk = kidx_ref[pl.ds(pl.multiple_of(c * kc, kc), kc), :]
        lg_ref[...] = _dot_nt(qs_ref[...], k_chunk)

        def score_tile(t, carry2):
            r0 = pl.multiple_of(t * rt, rt)
            sc = jnp.zeros((rt, kc), F32)
            for hh in range(idx_heads):
                lg = lg_ref[pl.ds(pl.multiple_of(hh * qb + r0, rt), rt), :]
                sc = sc + jnp.maximum(lg, 0.0) * pltpu.repeat(wb_ref[hh, pl.ds(r0, rt), :], lane_tiles, axis=1)
            q_pos = i * qb + r0 + lax.broadcasted_iota(jnp.int32, (rt, kc), 0)
            k_pos = c * kc + lax.broadcasted_iota(jnp.int32, (rt, kc), 1)
            keys_ref[c, pl.ds(r0, rt), :] = jnp.where(k_pos <= q_pos, _order_key(sc), jnp.int32(INT_MIN))
            return carry2

        lax.fori_loop(0, qb // rt, score_tile, 0)
        return carry

    lax.fori_loop(0, n_chunks, score_chunk, 0)

    def bit_step(b, t_cur):
        cand = t_cur + lax.shift_left(jnp.int32(1), jnp.int32(31) - b)
        cand_w = pltpu.repeat(cand, kc // LANES, axis=1)

        def count_chunk(c, acc):
            ge = jnp.where(keys_ref[c] >= cand_w, 1.0, 0.0)
            for t in range(kc // LANES):
                acc = acc + ge[:, t * LANES:(t + 1) * LANES]
            return acc

        acc = lax.fori_loop(0, n_chunks, count_chunk, jnp.zeros((qb, LANES), F32))
        cnt = jnp.sum(acc, axis=1, keepdims=True)
        return jnp.where(cnt >= float(topk), cand, t_cur)

    thr = lax.fori_loop(0, 32, bit_step, jnp.full((qb, LANES), INT_MIN, jnp.int32))
    thr_w = pltpu.repeat(jnp.maximum(thr, jnp.int32(INT_MIN + 1)), kc // LANES, axis=1)

    def to_bias(c, carry):
        bias = jnp.where(keys_ref[c] >= thr_w, 0.0, MASKED_SCORE).astype(F32)
        keys_ref[c] = pltpu.bitcast(bias, jnp.int32)
        return carry

    lax.fori_loop(0, n_chunks, to_bias, 0)

    m_ref[...] = jnp.full(m_ref.shape, MASKED_SCORE, F32)
    l_ref[...] = jnp.zeros(l_ref.shape, F32)
    acc_ref[...] = jnp.zeros(acc_ref.shape, F32)
    c_exp = sm_scale * math.log2(math.e)

    def unit(u):
        return u // n_groups, u % n_groups

    def kv_chunk(c):
        return ckv_ref[pl.ds(pl.multiple_of(c * kc, kc), kc), :]

    def logits(u, s_ref):
        c, g = unit(u)
        q = qlat_ref[0, pl.ds(g * grp, grp), :, :].reshape(rows_g, rank)
        s_ref[...] = _dot_nt(q, kv_chunk(c))

    def softmax(u, s_ref, p_ref, a_ref):
        c, g = unit(u)
        base = g * rows_g
        for t in range(qb // rt):
            bias = pltpu.bitcast(keys_ref[c, t * rt:(t + 1) * rt, :], F32)
            for hh in range(grp):
                lo = hh * qb + t * rt
                rs = pl.ds(pl.multiple_of(base + lo, rt), rt)
                s = s_ref[lo:lo + rt, :] + bias
                m_prev = m_ref[rs, :]
                m_next = jnp.maximum(m_prev, jnp.max(s, axis=1, keepdims=True))
                p = jnp.exp2((s - pltpu.repeat(m_next, lane_tiles, axis=1)) * c_exp)
                alpha = jnp.exp2((m_prev - m_next) * c_exp)
                p_sum = p[:, 0:LANES]
                for j in range(1, lane_tiles):
                    p_sum = p_sum + p[:, j * LANES:(j + 1) * LANES]
                l_ref[rs, :] = alpha * l_ref[rs, :] + p_sum
                m_ref[rs, :] = m_next
                a_ref[lo:lo + rt, :] = alpha
                p_ref[lo:lo + rt, :] = p.astype(BF16)

    def weighted_sum(u, p_ref, a_ref):
        c, g = unit(u)
        rs = pl.ds(pl.multiple_of(g * rows_g, rows_g), rows_g)
        acc_ref[rs, :] = (acc_ref[rs, :] * pltpu.repeat(a_ref[...], rank // LANES, axis=1)
                          + _dot(p_ref[...], kv_chunk(c)))

    n_units = n_chunks * n_groups
    logits(0, s0_ref)
    logits(1, s1_ref)
    softmax(0, s0_ref, p0_ref, a0_ref)

    def unit_pair(j, carry):
        u = 2 * j + 2
        logits(u, s0_ref)
        softmax(u - 1, s1_ref, p1_ref, a1_ref)
        weighted_sum(u - 2, p0_ref, a0_ref)
        logits(u + 1, s1_ref)
        softmax(u, s0_ref, p0_ref, a0_ref)
        weighted_sum(u - 1, p1_ref, a1_ref)
        return carry

    lax.fori_loop(0, (n_units - 2) // 2, unit_pair, 0)
    softmax(n_units - 1, s1_ref, p1_ref, a1_ref)
    weighted_sum(n_units - 2, p0_ref, a0_ref)
    weighted_sum(n_units - 1, p1_ref, a1_ref)

    l_row = jnp.sum(l_ref[...], axis=1, keepdims=True)
    o_ref[0] = (acc_ref[...] / l_row).astype(BF16).reshape(n_heads, qb, rank)


def _attention(qidx, widx, qlat4, kidx, ckv, *, idx_heads, idx_dim, topk, kc, sm_scale):
    nqb, n_heads, qb, rank = qlat4.shape
    tp = ckv.shape[0]
    assert n_heads % HEAD_GROUP == 0 and (n_heads // HEAD_GROUP) % 2 == 0 and qb % ROW_TILE == 0
    rows_g = HEAD_GROUP * qb
    body = functools.partial(_attn_body, idx_heads=idx_heads, idx_dim=idx_dim, topk=topk, kc=kc, sm_scale=sm_scale)
    return pl.pallas_call(
        body,
        grid=(nqb,),
        in_specs=[pl.BlockSpec((qb, idx_heads * LANES), lambda i: (i, 0)),
                  pl.BlockSpec((qb, LANES), lambda i: (i, 0)),
                  pl.BlockSpec((1, n_heads, qb, rank), lambda i: (i, 0, 0, 0)),
                  pl.BlockSpec((tp, LANES), lambda i: (0, 0)),
                  pl.BlockSpec((tp, rank), lambda i: (0, 0))],
        out_specs=pl.BlockSpec((1, n_heads, qb, rank), lambda i: (i, 0, 0, 0)),
        out_shape=jax.ShapeDtypeStruct((nqb, n_heads, qb, rank), BF16),
        scratch_shapes=[pltpu.VMEM((tp // kc, qb, kc), jnp.int32),
                        pltpu.VMEM((idx_heads, qb, LANES), F32),
                        pltpu.VMEM((idx_heads * qb, LANES), BF16),
                        pltpu.VMEM((idx_heads * qb, kc), F32),
                        pltpu.VMEM((rows_g, kc), F32),
                        pltpu.VMEM((rows_g, kc), F32),
                        pltpu.VMEM((rows_g, kc), BF16),
                        pltpu.VMEM((rows_g, kc), BF16),
                        pltpu.VMEM((rows_g, LANES), F32),
                        pltpu.VMEM((rows_g, LANES), F32),
                        pltpu.VMEM((n_heads * qb, LANES), F32),
                        pltpu.VMEM((n_heads * qb, LANES), F32),
                        pltpu.VMEM((n_heads * qb, rank), F32)],
        compiler_params=_cparams("arbitrary"),
        name="dsa_attention",
    )(qidx, widx, qlat4, kidx, ckv)


def _ov_body(olat_ref, wuv_ref, o_ref, *, head_dim, heads_per_step):
    nqb, _, qb, rank = olat_ref.shape
    for t in range(heads_per_step):
        x = olat_ref[:, t, :, :].reshape(nqb * qb, rank)
        o_ref[:, t * head_dim:(t + 1) * head_dim] = _dot(x, wuv_ref[t]).astype(BF16)


def _value_up(olat4, w_uv, *, tm):
    nqb, n_heads, qb, rank = olat4.shape
    head_dim = w_uv.shape[2]
    tp = nqb * qb
    hps = 2 if n_heads % 2 == 0 else 1
    body = functools.partial(_ov_body, head_dim=head_dim, heads_per_step=hps)
    return pl.pallas_call(
        body,
        grid=(tp // tm, n_heads // hps),
        in_specs=[pl.BlockSpec((tm // qb, hps, qb, rank), lambda i, j: (i, j, 0, 0)),
                  pl.BlockSpec((hps, rank, head_dim), lambda i, j: (j, 0, 0))],
        out_specs=pl.BlockSpec((tm, hps * head_dim), lambda i, j: (i, j)),
        out_shape=jax.ShapeDtypeStruct((tp, n_heads * head_dim), BF16),
        compiler_params=_cparams("arbitrary", "arbitrary"),
        name="value_up",
    )(olat4, w_uv)


def _pool_body(p_ref, w_ref, s_ref, o_ref, buf_ref, *, halo, group):
    i = pl.program_id(0)
    tm = p_ref.shape[0]

    @pl.when(i == 0)
    def _():
        buf_ref[0:halo, :] = jnp.zeros((halo, buf_ref.shape[1]), F32)

    @pl.when(i > 0)
    def _():
        buf_ref[0:halo, :] = buf_ref[tm:tm + halo, :]

    buf_ref[halo:halo + tm, :] = p_ref[...]
    t1 = i * tm + lax.broadcasted_iota(jnp.int32, (tm, 1), 0) + 1
    for g, win in enumerate(POOL_WINDOWS):
        cols = slice(g * group, (g + 1) * group)
        cur = buf_ref[halo:halo + tm, cols]
        wsum = cur
        for j in range(1, win):
            wsum = wsum + buf_ref[halo - j:halo - j + tm, cols]
        count = jnp.minimum(t1, win).astype(F32)
        y = (wsum / count - cur).astype(BF16)
        o_ref[:, cols] = (_dot(y, w_ref[g]) * s_ref[:, cols]).astype(BF16)


def _pool(pg, w_pool, scale, *, tm, col_block):
    tp = pg.shape[0]
    n_groups, group, _ = w_pool.shape
    d_pool = n_groups * group
    halo = max(POOL_WINDOWS)
    body = functools.partial(_pool_body, halo=halo, group=group)
    return pl.pallas_call(
        body,
        grid=(tp // tm,),
        in_specs=[pl.BlockSpec((tm, d_pool), lambda i: (i, col_block)),
                  pl.BlockSpec((n_groups, group, group), lambda i: (0, 0, 0)),
                  pl.BlockSpec((1, d_pool), lambda i: (0, 0))],
        out_specs=pl.BlockSpec((tm, d_pool), lambda i: (i, 0)),
        out_shape=jax.ShapeDtypeStruct((tp, d_pool), BF16),
        scratch_shapes=[pltpu.VMEM((tm + halo, d_pool), F32)],
        compiler_params=_cparams("arbitrary"),
        name="multiscale_pool",
    )(pg, w_pool, scale)


def _merge_body(a_ref, b_ref, ga_ref, gb_ref, x_ref, w_ref, o_ref, merged_ref):
    @pl.when(pl.program_id(1) == 0)
    def _():
        merged = jax.nn.sigmoid(ga_ref[...]) * a_ref[...] + jax.nn.sigmoid(gb_ref[...]) * b_ref[...]
        merged_ref[...] = merged.astype(BF16)

    o_ref[...] = x_ref[...] + _dot(merged_ref[...], w_ref[...])


def _merge_out(a, b, pg, x, w_out, *, tm, tn):
    tp, d = x.shape
    return pl.pallas_call(
        _merge_body,
        grid=(tp // tm, d // tn),
        in_specs=[pl.BlockSpec((tm, d), lambda i, j: (i, 0)), pl.BlockSpec((tm, d), lambda i, j: (i, 0)),
                  pl.BlockSpec((tm, d), lambda i, j: (i, 0)), pl.BlockSpec((tm, d), lambda i, j: (i, 1)),
                  pl.BlockSpec((tm, tn), lambda i, j: (i, j)), pl.BlockSpec((d, tn), lambda i, j: (0, j))],
        out_specs=pl.BlockSpec((tm, tn), lambda i, j: (i, j)),
        out_shape=jax.ShapeDtypeStruct((tp, d), F32),
        scratch_shapes=[pltpu.VMEM((tm, d), BF16)],
        compiler_params=_cparams("arbitrary", "arbitrary"),
        name="merge_out",
    )(a, b, pg, pg, x, w_out)


def _mlp_body(x_ref, g_ref, w1_ref, w2_ref, o_ref, h_ref):
    @pl.when(pl.program_id(1) == 0)
    def _():
        x = x_ref[...]
        h_ref[...] = _rms(x, g_ref[...]).astype(BF16)
        o_ref[...] = x

    u = jnp.maximum(_dot(h_ref[...], w1_ref[...]), 0.0)
    o_ref[...] += _dot((u * u).astype(BF16), w2_ref[...])


def _mlp(x, g, w1, w2, *, tm, tf):
    tp, d = x.shape
    d_ff = w1.shape[1]
    return pl.pallas_call(
        _mlp_body,
        grid=(tp // tm, d_ff // tf),
        in_specs=[pl.BlockSpec((tm, d), lambda i, f: (i, 0)), pl.BlockSpec((1, d), lambda i, f: (0, 0)),
                  pl.BlockSpec((d, tf), lambda i, f: (0, f)), pl.BlockSpec((tf, d), lambda i, f: (f, 0))],
        out_specs=pl.BlockSpec((tm, d), lambda i, f: (i, 0)),
        out_shape=jax.ShapeDtypeStruct((tp, d), F32),
        scratch_shapes=[pltpu.VMEM((tm, d), BF16)],
        compiler_params=_cparams("arbitrary", "arbitrary"),
        name="mlp_relu2",
    )(x, g, w1, w2)


def _final_norm_body(x_ref, g_ref, o_ref):
    o_ref[...] = _rms(x_ref[...], g_ref[...])


def _final_norm(x, g, *, tm):
    tp, d = x.shape
    return pl.pallas_call(
        _final_norm_body,
        grid=(tp // tm,),
        in_specs=[pl.BlockSpec((tm, d), lambda i: (i, 0)), pl.BlockSpec((1, d), lambda i: (0, 0))],
        out_specs=pl.BlockSpec((tm, d), lambda i: (i, 0)),
        out_shape=jax.ShapeDtypeStruct((tp, d), F32),
        compiler_params=_cparams("arbitrary"),
        name="final_norm",
    )(x, g)


def kernel(x, meta_tokens, norm_mix_g, w_in, kv_norm_g, idx_k_norm_g, w_uk, w_uv, w_attn_o, w_pool, pool_scale,
           w_pool_o, w_out, norm_mlp_g, w_mlp_in, w_mlp_out, final_norm_g):
    batch, seq, d = x.shape
    assert batch == 1
    depth = w_in.shape[0]
    n_meta = meta_tokens.shape[0]
    n_heads, kv_rank, head_dim = w_uk.shape[1:]
    d_attn = n_heads * head_dim
    idx_dim = idx_k_norm_g.shape[1]
    d_pool = pool_scale.shape[1]
    n_in = w_in.shape[2]
    idx_heads = (n_in - d_attn - kv_rank - idx_dim - d_pool - 2 * d) // (idx_dim + 1)
    assert d_attn + kv_rank + idx_heads * idx_dim + idx_dim + idx_heads + d_pool + 2 * d == n_in
    assert idx_dim <= LANES and idx_heads <= LANES and kv_rank % LANES == 0 and d % LANES == 0
    assert d_pool % d == 0 or d % d_pool == 0

    n_keys = seq + n_meta
    tp = -(-n_keys // Q_BLOCK) * Q_BLOCK
    topk = min(TOPK_MAX, n_keys // 4)
    qb = Q_BLOCK
    tm = _pick_block(tp, 640, qb)
    kc = _pick_block(tp, 640, LANES)
    tn = _pick_block(d, 512, LANES)
    tf = _pick_block(w_mlp_in.shape[2], 512, LANES)

    h_res = jnp.concatenate([meta_tokens.astype(x.dtype), x[0], jnp.zeros((tp - n_keys, d), x.dtype)], axis=0)

    o_q, o_kv = 0, d_attn
    o_qi = o_kv + kv_rank
    o_ki = o_qi + idx_heads * idx_dim
    o_wi = o_ki + idx_dim
    o_p = o_wi + idx_heads
    o_ga = o_p + d_pool
    o_gb = o_ga + d

    for l in range(depth):
        wl = w_in[l]
        qi = wl[:, o_qi:o_ki].reshape(d, idx_heads, idx_dim)
        qi = jnp.pad(qi, ((0, 0), (0, 0), (0, LANES - idx_dim))).reshape(d, idx_heads * LANES)
        ki = jnp.pad(wl[:, o_ki:o_wi], ((0, 0), (0, LANES - idx_dim)))
        wi = jnp.pad(wl[:, o_wi:o_p], ((0, 0), (0, LANES - idx_heads)))
        w_small = jnp.concatenate([wl[:, o_kv:o_qi], qi, ki, wi], axis=1).astype(BF16)
        w_q = wl[:, o_q:o_kv].astype(BF16)
        w_pg = jnp.concatenate([wl[:, o_ga:o_gb], wl[:, o_gb:], wl[:, o_p:o_ga]], axis=1).astype(BF16)
        kig_pad = jnp.pad(idx_k_norm_g[l], (0, LANES - idx_dim))[None]

        h, ckv, qidx, kidx, widx = _proj_small(
            h_res, norm_mix_g[l][None], w_small, kv_norm_g[l][None], kig_pad,
            tm=tm, kv_rank=kv_rank, idx_heads=idx_heads, idx_dim=idx_dim)
        qlat4 = _qlat(h, w_q, w_uk[l].astype(BF16), tm=tm, qb=qb)
        pg = _mm(h, w_pg, tm=tm, out_dtype=F32, name="proj_gates_pool")

        olat4 = _attention(qidx, widx, qlat4, kidx, ckv, idx_heads=idx_heads, idx_dim=idx_dim, topk=topk,
                           kc=kc, sm_scale=head_dim ** -0.5)
        o = _value_up(olat4, w_uv[l].astype(BF16), tm=tm)
        a = _mm(o, w_attn_o[l].astype(BF16), tm=tm, out_dtype=F32, name="attn_out")

        yps = _pool(pg, w_pool[l].astype(BF16), pool_scale[l][None], tm=tm, col_block=(2 * d) // d_pool)
        b = _mm(yps, w_pool_o[l].astype(BF16), tm=tm, out_dtype=F32, name="pool_out")

        h_res = _merge_out(a, b, pg, h_res, w_out[l].astype(BF16), tm=tm, tn=tn)
        h_res = _mlp(h_res, norm_mlp_g[l][None], w_mlp_in[l].astype(BF16), w_mlp_out[l].astype(BF16), tm=tm, tf=tf)

    y = _final_norm(h_res, final_norm_g[None], tm=tm)
    return y[n_meta:n_meta + seq][None]
```

```python
import functools
import math

import jax
import jax.numpy as jnp
from jax import lax
from jax.experimental import pallas as pl
from jax.experimental.pallas import tpu as pltpu

EPS = 1e-6
TOPK_MAX = 256
Q_BLOCK = 128
POOL_WINDOWS = (2, 4, 8, 16)
LANES = 128
ROW_TILE = 32
HEAD_GROUP = 4
INT_MIN = -(2 ** 31)
MASKED_SCORE = -1e30
VMEM_LIMIT_BYTES = 56 * 1024 * 1024

BF16 = jnp.bfloat16
F32 = jnp.float32


def _cparams(*semantics):
    return pltpu.CompilerParams(dimension_semantics=semantics, vmem_limit_bytes=VMEM_LIMIT_BYTES)


def _rms(x, g):
    return x * lax.rsqrt(jnp.mean(x * x, axis=-1, keepdims=True) + EPS) * g


def _dot(a, b):
    return jnp.dot(a, b, preferred_element_type=F32)


def _dot_nt(a, b):
    return lax.dot_general(a, b, (((1,), (1,)), ((), ())), preferred_element_type=F32)


def _pick_block(total, target, quantum):
    best = None
    for cand in range(quantum, total + 1, quantum):
        if total % cand == 0 and cand <= target:
            best = cand
    assert best is not None, (total, target, quantum)
    return best


def _proj_small_body(x_ref, g_ref, w_ref, kvg_ref, kig_ref, h_ref, ckv_ref, qidx_ref, kidx_ref, widx_ref,
                     *, kv_rank, idx_heads, idx_dim):
    h = _rms(x_ref[...], g_ref[...]).astype(BF16)
    h_ref[...] = h
    pr = _dot(h, w_ref[...])
    ckv_ref[...] = _rms(pr[:, :kv_rank], kvg_ref[...]).astype(BF16)
    q_end = kv_rank + idx_heads * LANES
    qidx_ref[...] = pr[:, kv_rank:q_end].astype(BF16)
    kraw = pr[:, q_end:q_end + LANES]
    ms = jnp.sum(kraw * kraw, axis=-1, keepdims=True) * (1.0 / idx_dim)
    kidx_ref[...] = (kraw * lax.rsqrt(ms + EPS) * kig_ref[...]).astype(BF16)
    widx_ref[...] = pr[:, q_end + LANES:q_end + 2 * LANES] * (idx_heads ** -0.5)


def _proj_small(x, g, w_small, kvg, kig_pad, *, tm, kv_rank, idx_heads, idx_dim):
    tp, d = x.shape
    ns = w_small.shape[1]
    body = functools.partial(_proj_small_body, kv_rank=kv_rank, idx_heads=idx_heads, idx_dim=idx_dim)
    row = lambda i: (i, 0)
    const = lambda i: (0, 0)
    return pl.pallas_call(
        body,
        grid=(tp // tm,),
        in_specs=[pl.BlockSpec((tm, d), row), pl.BlockSpec((1, d), const), pl.BlockSpec((d, ns), const),
                  pl.BlockSpec((1, kv_rank), const), pl.BlockSpec((1, LANES), const)],
        out_specs=[pl.BlockSpec((tm, d), row), pl.BlockSpec((tm, kv_rank), row),
                   pl.BlockSpec((tm, idx_heads * LANES), row), pl.BlockSpec((tm, LANES), row),
                   pl.BlockSpec((tm, LANES), row)],
        out_shape=[jax.ShapeDtypeStruct((tp, d), BF16), jax.ShapeDtypeStruct((tp, kv_rank), BF16),
                   jax.ShapeDtypeStruct((tp, idx_heads * LANES), BF16), jax.ShapeDtypeStruct((tp, LANES), BF16),
                   jax.ShapeDtypeStruct((tp, LANES), F32)],
        compiler_params=_cparams("arbitrary"),
        name="proj_small",
    )(x, g, w_small, kvg, kig_pad)


def _qlat_body(h_ref, wq_ref, wuk_ref, o_ref, *, head_dim, heads_per_step):
    q = _dot(h_ref[...], wq_ref[...]).astype(BF16)
    nqb, _, qb, rank = o_ref.shape
    for t in range(heads_per_step):
        ql = _dot_nt(q[:, t * head_dim:(t + 1) * head_dim], wuk_ref[t])
        o_ref[:, t, :, :] = ql.astype(BF16).reshape(nqb, qb, rank)


def _qlat(h, wq, w_uk, *, tm, qb):
    tp, d = h.shape
    n_heads, rank, head_dim = w_uk.shape
    hps = max(c for c in (4, 2, 1) if n_heads % c == 0)
    body = functools.partial(_qlat_body, head_dim=head_dim, heads_per_step=hps)
    return pl.pallas_call(
        body,
        grid=(tp // tm, n_heads // hps),
        in_specs=[pl.BlockSpec((tm, d), lambda i, j: (i, 0)),
                  pl.BlockSpec((d, hps * head_dim), lambda i, j: (0, j)),
                  pl.BlockSpec((hps, rank, head_dim), lambda i, j: (j, 0, 0))],
        out_specs=pl.BlockSpec((tm // qb, hps, qb, rank), lambda i, j: (i, j, 0, 0)),
        out_shape=jax.ShapeDtypeStruct((tp // qb, n_heads, qb, rank), BF16),
        compiler_params=_cparams("arbitrary", "arbitrary"),
        name="q_latent",
    )(h, wq, w_uk)


def _order_key(score):
    bits = pltpu.bitcast(score, jnp.int32)
    return bits ^ ((bits >> 31) & jnp.int32(0x7FFFFFFF))


def _attn_body(qidx_ref, widx_ref, qlat_ref, kidx_ref, ckv_ref, o_ref,
               keys_ref, wb_ref, qs_ref, lg_ref, s0_ref, s1_ref, p0_ref, p1_ref, a0_ref, a1_ref,
               m_ref, l_ref, acc_ref, *, idx_heads, idx_dim, topk, kc, sm_scale):
    i = pl.program_id(0)
    n_heads, qb, rank = qlat_ref.shape[1:]
    lane_tiles = kc // LANES
    rt = ROW_TILE
    grp = HEAD_GROUP
    n_groups = n_heads // grp
    rows_g = grp * qb
    n_chunks = ((i + 1) * qb + (kc - 1)) // kc

    w_all = widx_ref[...] * (idx_dim ** -0.5)
    for hh in range(idx_heads):
        wb_ref[hh] = jnp.broadcast_to(w_all[:, hh:hh + 1], (qb, LANES))
        qs_ref[hh * qb:(hh + 1) * qb, :] = qidx_ref[:, hh * LANES:(hh + 1) * LANES]

    def score_chunk(c, carry):
        k_chunk = kidx_ref[pl.ds(pl.multiple_of(c * kc, kc), kc), :]
        lg_ref[...] = _dot_nt(qs_ref[...], k_chunk)

        def score_tile(t, carry2):
            r0 = pl.multiple_of(t * rt, rt)
            sc = jnp.zeros((rt, kc), F32)
            for hh in range(idx_heads):
                lg = lg_ref[pl.ds(pl.multiple_of(hh * qb + r0, rt), rt), :]
                sc = sc + jnp.maximum(lg, 0.0) * pltpu.repeat(wb_ref[hh, pl.ds(r0, rt), :], lane_tiles, axis=1)
            q_pos = i * qb + r0 + lax.broadcasted_iota(jnp.int32, (rt, kc), 0)
            k_pos = c * kc + lax.broadcasted_iota(jnp.int32, (rt, kc), 1)
            keys_ref[c, pl.ds(r0, rt), :] = jnp.where(k_pos <= q_pos, _order_key(sc), jnp.int32(INT_MIN))
            return carry2

        lax.fori_loop(0, qb // rt, score_tile, 0)
        return carry

    lax.fori_loop(0, n_chunks, score_chunk, 0)

    def bit_step(b, t_cur):
        cand = t_cur + lax.shift_left(jnp.int32(1), jnp.int32(31) - b)
        cand_w = pltpu.repeat(cand, kc // LANES, axis=1)

        def count_chunk(c, acc):
            ge = jnp.where(keys_ref[c] >= cand_w, 1.0, 0.0)
            for t in range(kc // LANES):
                acc = acc + ge[:, t * LANES:(t + 1) * LANES]
            return acc

        acc = lax.fori_loop(0, n_chunks, count_chunk, jnp.zeros((qb, LANES), F32))
        cnt = jnp.sum(acc, axis=1, keepdims=True)
        return jnp.where(cnt >= float(topk), cand, t_cur)

    thr = lax.fori_loop(0, 32, bit_step, jnp.full((qb, LANES), INT_MIN, jnp.int32))
    thr_w = pltpu.repeat(jnp.maximum(thr, jnp.int32(INT_MIN + 1)), kc // LANES, axis=1)

    def to_bias(c, carry):
        bias = jnp.where(keys_ref[c] >= thr_w, 0.0, MASKED_SCORE).astype(F32)
        keys_ref[c] = pltpu.bitcast(bias, jnp.int32)
        return carry

    lax.fori_loop(0, n_chunks, to_bias, 0)

    m_ref[...] = jnp.full(m_ref.shape, MASKED_SCORE, F32)
    l_ref[...] = jnp.zeros(l_ref.shape, F32)
    acc_ref[...] = jnp.zeros(acc_ref.shape, F32)
    c_exp = sm_scale * math.log2(math.e)

    def unit(u):
        return u // n_groups, u % n_groups

    def kv_chunk(c):
        return ckv_ref[pl.ds(pl.multiple_of(c * kc, kc), kc), :]

    def logits(u, s_ref):
        c, g = unit(u)
        q = qlat_ref[0, pl.ds(g * grp, grp), :, :].reshape(rows_g, rank)
        s_ref[...] = _dot_nt(q, kv_chunk(c))

    def softmax(u, s_ref, p_ref, a_ref):
        c, g = unit(u)
        base = g * rows_g
        for t in range(qb // rt):
            bias = pltpu.bitcast(keys_ref[c, t * rt:(t + 1) * rt, :], F32)
            for hh in range(grp):
                lo = hh * qb + t * rt
                rs = pl.ds(pl.multiple_of(base + lo, rt), rt)
                s = s_ref[lo:lo + rt, :] + bias
                m_prev = m_ref[rs, :]
                m_next = jnp.maximum(m_prev, jnp.max(s, axis=1, keepdims=True))
                p = jnp.exp2((s - pltpu.repeat(m_next, lane_tiles, axis=1)) * c_exp)
                alpha = jnp.exp2((m_prev - m_next) * c_exp)
                p_sum = p[:, 0:LANES]
                for j in range(1, lane_tiles):
                    p_sum = p_sum + p[:, j * LANES:(j + 1) * LANES]
                l_ref[rs, :] = alpha * l_ref[rs, :] + p_sum
                m_ref[rs, :] = m_next
                a_ref[lo:lo + rt, :] = alpha
                p_ref[lo:lo + rt, :] = p.astype(BF16)

    def weighted_sum(u, p_ref, a_ref):
        c, g = unit(u)
        rs = pl.ds(pl.multiple_of(g * rows_g, rows_g), rows_g)
        acc_ref[rs, :] = (acc_ref[rs, :] * pltpu.repeat(a_ref[...], rank // LANES, axis=1)
                          + _dot(p_ref[...], kv_chunk(c)))

    n_units = n_chunks * n_groups
    logits(0, s0_ref)
    logits(1, s1_ref)
    softmax(0, s0_ref, p0_ref, a0_ref)

    def unit_pair(j, carry):
        u = 2 * j + 2
        logits(u, s0_ref)
        softmax(u - 1, s1_ref, p1_ref, a1_ref)
        weighted_sum(u - 2, p0_ref, a0_ref)
        logits(u + 1, s1_ref)
        softmax(u, s0_ref, p0_ref, a0_ref)
        weighted_sum(u - 1, p1_ref, a1_ref)
        return carry

    lax.fori_loop(0, (n_units - 2) // 2, unit_pair, 0)
    softmax(n_units - 1, s1_ref, p1_ref, a1_ref)
    weighted_sum(n_units - 2, p0_ref, a0_ref)
    weighted_sum(n_units - 1, p1_ref, a1_ref)

    l_row = jnp.sum(l_ref[...], axis=1, keepdims=True)
    o_ref[0] = (acc_ref[...] / l_row).astype(BF16).reshape(n_heads, qb, rank)


def _attention(qidx, widx, qlat4, kidx, ckv, *, idx_heads, idx_dim, topk, kc, sm_scale):
    nqb, n_heads, qb, rank = qlat4.shape
    tp = ckv.shape[0]
    assert n_heads % HEAD_GROUP == 0 and (n_heads // HEAD_GROUP) % 2 == 0 and qb % ROW_TILE == 0
    rows_g = HEAD_GROUP * qb
    body = functools.partial(_attn_body, idx_heads=idx_heads, idx_dim=idx_dim, topk=topk, kc=kc, sm_scale=sm_scale)
    return pl.pallas_call(
        body,
        grid=(nqb,),
        in_specs=[pl.BlockSpec((qb, idx_heads * LANES), lambda i: (i, 0)),
                  pl.BlockSpec((qb, LANES), lambda i: (i, 0)),
                  pl.BlockSpec((1, n_heads, qb, rank), lambda i: (i, 0, 0, 0)),
                  pl.BlockSpec((tp, LANES), lambda i: (0, 0)),
                  pl.BlockSpec((tp, rank), lambda i: (0, 0))],
        out_specs=pl.BlockSpec((1, n_heads, qb, rank), lambda i: (i, 0, 0, 0)),
        out_shape=jax.ShapeDtypeStruct((nqb, n_heads, qb, rank), BF16),
        scratch_shapes=[pltpu.VMEM((tp // kc, qb, kc), jnp.int32),
                        pltpu.VMEM((idx_heads, qb, LANES), F32),
                        pltpu.VMEM((idx_heads * qb, LANES), BF16),
                        pltpu.VMEM((idx_heads * qb, kc), F32),
                        pltpu.VMEM((rows_g, kc), F32),
                        pltpu.VMEM((rows_g, kc), F32),
                        pltpu.VMEM((rows_g, kc), BF16),
                        pltpu.VMEM((rows_g, kc), BF16),
                        pltpu.VMEM((rows_g, LANES), F32),
                        pltpu.VMEM((rows_g, LANES), F32),
                        pltpu.VMEM((n_heads * qb, LANES), F32),
                        pltpu.VMEM((n_heads * qb, LANES), F32),
                        pltpu.VMEM((n_heads * qb, rank), F32)],
        compiler_params=_cparams("arbitrary"),
        name="dsa_attention",
    )(qidx, widx, qlat4, kidx, ckv)


def _pool_body(h_ref, wp_ref, w_ref, s_ref, o_ref, buf_ref, *, halo, group):
    i = pl.program_id(0)
    tm = h_ref.shape[0]

    @pl.when(i == 0)
    def _():
        buf_ref[0:halo, :] = jnp.zeros((halo, buf_ref.shape[1]), F32)

    @pl.when(i > 0)
    def _():
        buf_ref[0:halo, :] = buf_ref[tm:tm + halo, :]

    buf_ref[halo:halo + tm, :] = _dot(h_ref[...], wp_ref[...])
    t1 = i * tm + lax.broadcasted_iota(jnp.int32, (tm, 1), 0) + 1
    for g, win in enumerate(POOL_WINDOWS):
        cols = slice(g * group, (g + 1) * group)
        cur = buf_ref[halo:halo + tm, cols]
        wsum = cur
        for j in range(1, win):
            wsum = wsum + buf_ref[halo - j:halo - j + tm, cols]
        count = jnp.minimum(t1, win).astype(F32)
        y = (wsum / count - cur).astype(BF16)
        o_ref[:, cols] = (_dot(y, w_ref[g]) * s_ref[:, cols]).astype(BF16)


def _pool(h, w_pgg, w_pool, scale, *, tm):
    tp, d = h.shape
    n_groups, group, _ = w_pool.shape
    d_pool = n_groups * group
    halo = max(POOL_WINDOWS)
    body = functools.partial(_pool_body, halo=halo, group=group)
    return pl.pallas_call(
        body,
        grid=(tp // tm,),
        in_specs=[pl.BlockSpec((tm, d), lambda i: (i, 0)),
                  pl.BlockSpec((d, d_pool), lambda i: (0, 0)),
                  pl.BlockSpec((n_groups, group, group), lambda i: (0, 0, 0)),
                  pl.BlockSpec((1, d_pool), lambda i: (0, 0))],
        out_specs=pl.BlockSpec((tm, d_pool), lambda i: (i, 0)),
        out_shape=jax.ShapeDtypeStruct((tp, d_pool), BF16),
        scratch_shapes=[pltpu.VMEM((tm + halo, d_pool), F32)],
        compiler_params=_cparams("arbitrary"),
        name="multiscale_pool",
    )(h, w_pgg, w_pool, scale)


def _tail_body(olat_ref, wuv_ref, wo_ref, y_ref, wpo_ref, h_ref, wga_ref, wgb_ref, x_ref, wout_ref, o_ref,
               o_scr, merged_scr, *, head_dim):
    j = pl.program_id(1)
    nc, tm, tn = merged_scr.shape
    nqb, n_heads, qb, rank = olat_ref.shape

    @pl.when(j == 0)
    def _():
        for hh in range(n_heads):
            x = olat_ref[:, hh, :, :].reshape(nqb * qb, rank)
            o_scr[:, hh * head_dim:(hh + 1) * head_dim] = _dot(x, wuv_ref[hh]).astype(BF16)

    @pl.when(j < nc)
    def _():
        h = h_ref[...]
        a = _dot(o_scr[...], wo_ref[...])
        b = _dot(y_ref[...], wpo_ref[...])
        merged = jax.nn.sigmoid(_dot(h, wga_ref[...])) * a + jax.nn.sigmoid(_dot(h, wgb_ref[...])) * b
        merged_scr[j] = merged.astype(BF16)

    @pl.when(j >= nc)
    def _():
        acc = x_ref[...]
        for c in range(nc):
            acc = acc + _dot(merged_scr[c], wout_ref[c * tn:(c + 1) * tn, :])
        o_ref[...] = acc


def _branch_tail(olat4, w_uv, w_attn_o, yps, w_pool_o, h, w_pgg, x, w_out, *, tm, tn):
    tp, d = x.shape
    nqb, n_heads, qb, rank = olat4.shape
    head_dim = w_uv.shape[2]
    d_attn = n_heads * head_dim
    d_pool = yps.shape[1]
    assert d % tn == 0 and d_pool % tn == 0
    nc = d // tn
    ga0, gb0 = d_pool // tn, (d_pool + d) // tn
    merged_col = lambda i, j: (0, jnp.minimum(j, nc - 1))
    out_col = lambda i, j: (i, jnp.maximum(j - nc, 0))
    body = functools.partial(_tail_body, head_dim=head_dim)
    return pl.pallas_call(
        body,
        grid=(tp // tm, 2 * nc),
        in_specs=[pl.BlockSpec((tm // qb, n_heads, qb, rank), lambda i, j: (i, 0, 0, 0)),
                  pl.BlockSpec((n_heads, rank, head_dim), lambda i, j: (0, 0, 0)),
                  pl.BlockSpec((d_attn, tn), merged_col),
                  pl.BlockSpec((tm, d_pool), lambda i, j: (i, 0)),
                  pl.BlockSpec((d_pool, tn), merged_col),
                  pl.BlockSpec((tm, d), lambda i, j: (i, 0)),
                  pl.BlockSpec((d, tn), lambda i, j: (0, ga0 + jnp.minimum(j, nc - 1))),
                  pl.BlockSpec((d, tn), lambda i, j: (0, gb0 + jnp.minimum(j, nc - 1))),
                  pl.BlockSpec((tm, tn), out_col),
                  pl.BlockSpec((d, tn), lambda i, j: (0, jnp.maximum(j - nc, 0)))],
        out_specs=pl.BlockSpec((tm, tn), out_col),
        out_shape=jax.ShapeDtypeStruct((tp, d), F32),
        scratch_shapes=[pltpu.VMEM((tm, d_attn), BF16), pltpu.VMEM((nc, tm, tn), BF16)],
        compiler_params=_cparams("arbitrary", "arbitrary"),
        name="branch_tail",
    )(olat4, w_uv, w_attn_o, yps, w_pool_o, h, w_pgg, w_pgg, x, w_out)


def _mlp_body(x_ref, g_ref, w1_ref, w2_ref, o_ref, h_ref):
    @pl.when(pl.program_id(1) == 0)
    def _():
        x = x_ref[...]
        h_ref[...] = _rms(x, g_ref[...]).astype(BF16)
        o_ref[...] = x

    u = jnp.maximum(_dot(h_ref[...], w1_ref[...]), 0.0)
    o_ref[...] += _dot((u * u).astype(BF16), w2_ref[...])


def _mlp(x, g, w1, w2, *, tm, tf):
    tp, d = x.shape
    d_ff = w1.shape[1]
    return pl.pallas_call(
        _mlp_body,
        grid=(tp // tm, d_ff // tf),
        in_specs=[pl.BlockSpec((tm, d), lambda i, f: (i, 0)), pl.BlockSpec((1, d), lambda i, f: (0, 0)),
                  pl.BlockSpec((d, tf), lambda i, f: (0, f)), pl.BlockSpec((tf, d), lambda i, f: (f, 0))],
        out_specs=pl.BlockSpec((tm, d), lambda i, f: (i, 0)),
        out_shape=jax.ShapeDtypeStruct((tp, d), F32),
        scratch_shapes=[pltpu.VMEM((tm, d), BF16)],
        compiler_params=_cparams("arbitrary", "arbitrary"),
        name="mlp_relu2",
    )(x, g, w1, w2)


def _final_norm_body(x_ref, g_ref, o_ref):
    o_ref[...] = _rms(x_ref[...], g_ref[...])


def _final_norm(x, g, *, tm):
    tp, d = x.shape
    return pl.pallas_call(
        _final_norm_body,
        grid=(tp // tm,),
        in_specs=[pl.BlockSpec((tm, d), lambda i: (i, 0)), pl.BlockSpec((1, d), lambda i: (0, 0))],
        out_specs=pl.BlockSpec((tm, d), lambda i: (i, 0)),
        out_shape=jax.ShapeDtypeStruct((tp, d), F32),
        compiler_params=_cparams("arbitrary"),
        name="final_norm",
    )(x, g)


def kernel(x, meta_tokens, norm_mix_g, w_in, kv_norm_g, idx_k_norm_g, w_uk, w_uv, w_attn_o, w_pool, pool_scale,
           w_pool_o, w_out, norm_mlp_g, w_mlp_in, w_mlp_out, final_norm_g):
    batch, seq, d = x.shape
    assert batch == 1
    depth = w_in.shape[0]
    n_meta = meta_tokens.shape[0]
    n_heads, kv_rank, head_dim = w_uk.shape[1:]
    d_attn = n_heads * head_dim
    idx_dim = idx_k_norm_g.shape[1]
    d_pool = pool_scale.shape[1]
    n_in = w_in.shape[2]
    idx_heads = (n_in - d_attn - kv_rank - idx_dim - d_pool - 2 * d) // (idx_dim + 1)
    assert d_attn + kv_rank + idx_heads * idx_dim + idx_dim + idx_heads + d_pool + 2 * d == n_in
    assert idx_dim <= LANES and idx_heads <= LANES and kv_rank % LANES == 0 and d % LANES == 0
    assert d_pool % d == 0 or d % d_pool == 0

    n_keys = seq + n_meta
    tp = -(-n_keys // Q_BLOCK) * Q_BLOCK
    topk = min(TOPK_MAX, n_keys // 4)
    qb = Q_BLOCK
    tm = _pick_block(tp, 640, qb)
    kc = _pick_block(tp, 640, LANES)
    tn = _pick_block(math.gcd(d, d_pool), 512, LANES)
    tf = _pick_block(w_mlp_in.shape[2], 512, LANES)

    h_res = jnp.concatenate([meta_tokens.astype(x.dtype), x[0], jnp.zeros((tp - n_keys, d), x.dtype)], axis=0)

    o_q, o_kv = 0, d_attn
    o_qi = o_kv + kv_rank
    o_ki = o_qi + idx_heads * idx_dim
    o_wi = o_ki + idx_dim
    o_p = o_wi + idx_heads
    o_ga = o_p + d_pool
    o_gb = o_ga + d

    for l in range(depth):
        wl = w_in[l]
        qi = wl[:, o_qi:o_ki].reshape(d, idx_heads, idx_dim)
        qi = jnp.pad(qi, ((0, 0), (0, 0), (0, LANES - idx_dim))).reshape(d, idx_heads * LANES)
        ki = jnp.pad(wl[:, o_ki:o_wi], ((0, 0), (0, LANES - idx_dim)))
        wi = jnp.pad(wl[:, o_wi:o_p], ((0, 0), (0, LANES - idx_heads)))
        w_small = jnp.concatenate([wl[:, o_kv:o_qi], qi, ki, wi], axis=1).astype(BF16)
        w_q = wl[:, o_q:o_kv].astype(BF16)
        w_pgg = wl[:, o_p:].astype(BF16)
        kig_pad = jnp.pad(idx_k_norm_g[l], (0, LANES - idx_dim))[None]

        h, ckv, qidx, kidx, widx = _proj_small(
            h_res, norm_mix_g[l][None], w_small, kv_norm_g[l][None], kig_pad,
            tm=tm, kv_rank=kv_rank, idx_heads=idx_heads, idx_dim=idx_dim)
        qlat4 = _qlat(h, w_q, w_uk[l].astype(BF16), tm=tm, qb=qb)
        olat4 = _attention(qidx, widx, qlat4, kidx, ckv, idx_heads=idx_heads, idx_dim=idx_dim, topk=topk,
                           kc=kc, sm_scale=head_dim ** -0.5)
        yps = _pool(h, w_pgg, w_pool[l].astype(BF16), pool_scale[l][None], tm=tm)
        h_res = _branch_tail(olat4, w_uv[l].astype(BF16), w_attn_o[l].astype(BF16), yps, w_pool_o[l].astype(BF16),
                             h, w_pgg, h_res, w_out[l].astype(BF16), tm=tm, tn=tn)
        h_res = _mlp(h_res, norm_mlp_g[l][None], w_mlp_in[l].astype(BF16), w_mlp_out[l].astype(BF16), tm=tm, tf=tf)

    y = _final_norm(h_res, final_norm_g[None], tm=tm)
    return y[n_meta:n_meta + seq][None]
```
